```python
import jax
import jax.numpy as jnp
from jax import lax
import numpy as np

D_MODEL = 1024
BATCH = 2
SEQ = 8192
DEPTH = 2
DEC_BATCH = 32
DEC_SEQ = 4
PAST_LEN = 8192
PAGE_SIZE = 128

HEAD_DIM = 64
MIX_W = D_MODEL
ATT_W = MIX_W // 2
SB_HEADS = ATT_W // HEAD_DIM
SB_BIAS_INIT = -7.0
CONV_CH = MIX_W - ATT_W
CONV_WIDTH = 31
IN_W = 3 * ATT_W + 2 * CONV_CH
Q_BLOCK = 128
N_EXPERTS = 16
N_EXPERT_GROUPS = 4
EXPERTS_PER_GROUP = N_EXPERTS // N_EXPERT_GROUPS
TOP_K = 2
D_EXPERT = D_MODEL // 2
EPS = 1e-6

kernel_name = 'hymba_stickbreak_conformer_moe_step'


def rmsnorm(x, g):
    xf = x.astype(jnp.float32)
    y = xf * lax.rsqrt(jnp.mean(xf * xf, axis=-1, keepdims=True) + EPS)
    return (y * g.astype(jnp.float32)).astype(x.dtype)


def layernorm(x, g, b):
    xf = x.astype(jnp.float32)
    xc = xf - jnp.mean(xf, axis=-1, keepdims=True)
    y = xc * lax.rsqrt(jnp.mean(xc * xc, axis=-1, keepdims=True) + EPS)
    return (y * g.astype(jnp.float32) + b.astype(jnp.float32)).astype(x.dtype)


def adaln(c, w, b):
    m = jnp.einsum('bd,de->be', jax.nn.silu(c), w) + b
    return [t[:, None, :] for t in jnp.split(m, 6, axis=-1)]


def modulate(h, shift, scale):
    return h * (1 + scale) + shift


def mix_project(h, w_in_l, qn, kn):
    bsz, n = h.shape[:2]
    u = jnp.einsum('bnd,de->bne', h, w_in_l)
    q, k, v, ca, cg = jnp.split(u, [ATT_W, 2 * ATT_W, 3 * ATT_W, 3 * ATT_W + CONV_CH], axis=-1)
    heads = lambda t: t.reshape(bsz, n, SB_HEADS, HEAD_DIM)
    q = rmsnorm(heads(q), qn)
    k = rmsnorm(heads(k), kn)
    glu = ca * jax.nn.sigmoid(cg)
    return q, k, heads(v), glu


def stick_breaking(q, k, v, q_pos, k_pos, bias):
    z = jnp.einsum('bqhd,bkhd->bhqk', q, k, preferred_element_type=jnp.float32) * (HEAD_DIM ** -0.5)
    z = z + bias.astype(jnp.float32)[None, :, None, None]
    mask = k_pos[None, :] < q_pos[:, None]
    log_keep = jnp.where(mask, jax.nn.log_sigmoid(-z), 0.0)
    log_between = lax.cumsum(log_keep, axis=3, reverse=True) - log_keep
    a = jnp.where(mask, jnp.exp(jax.nn.log_sigmoid(z) + log_between), 0.0)
    return jnp.einsum('bhqk,bkhd->bqhd', a.astype(v.dtype), v)


def sb_prompt(q, k, v, bias):
    bsz, n = q.shape[:2]
    nb = n // Q_BLOCK
    pos = jnp.arange(n)
    qb = q.reshape(bsz, nb, Q_BLOCK, SB_HEADS, HEAD_DIM).swapaxes(0, 1)
    pb = pos.reshape(nb, Q_BLOCK)
    ob = lax.map(lambda a: stick_breaking(a[0], k, v, a[1], pos, bias), (qb, pb))
    return ob.swapaxes(0, 1).reshape(bsz, n, SB_HEADS, HEAD_DIM)


def conv_module(glu_ext, w, b, ln_g, ln_b):
    y = lax.conv_general_dilated(glu_ext, w[:, None, :].astype(glu_ext.dtype), (1,), 'VALID',
                                 dimension_numbers=('NWC', 'WIO', 'NWC'),
                                 feature_group_count=CONV_CH) + b
    return jax.nn.silu(layernorm(y, ln_g, ln_b))


def mix_merge(att, cv, w_out_l):
    bsz, n = att.shape[:2]
    cat = jnp.concatenate([att.reshape(bsz, n, ATT_W), cv], axis=-1)
    return jnp.einsum('bne,ed->bnd', cat, w_out_l)


def moe(h, w_router, b_router, wg, wu, wd):
    shape = h.shape
    t = h.reshape(-1, D_MODEL)
    n = t.shape[0]
    logits = jnp.einsum('nd,de->ne', t, w_router, preferred_element_type=jnp.float32)
    probs = jax.nn.softmax(logits, axis=-1)
    sel = probs + b_router.astype(jnp.float32)
    grp_score = lax.top_k(sel.reshape(n, N_EXPERT_GROUPS, EXPERTS_PER_GROUP), TOP_K)[0].sum(-1)
    best = jnp.argmax(grp_score, axis=-1)
    in_grp = (jnp.arange(N_EXPERTS) // EXPERTS_PER_GROUP)[None, :] == best[:, None]
    _, idx = lax.top_k(jnp.where(in_grp, sel, -jnp.inf), TOP_K)
    p_sel = jnp.take_along_axis(probs, idx, axis=-1)
    w = p_sel / jnp.sum(p_sel, axis=-1, keepdims=True)
    gates = jnp.sum(jax.nn.one_hot(idx, N_EXPERTS, dtype=jnp.float32) * w[..., None], axis=1)
    a = jnp.einsum('nd,edf->nef', t, wg)
    b = jnp.einsum('nd,edf->nef', t, wu)
    act = jax.nn.silu(a) * b * gates[:, :, None].astype(t.dtype)
    return jnp.einsum('nef,efd->nd', act, wd).reshape(shape)


def ffn_sublayer(x, shift, scale, gate, g, w_router, b_router, wg, wu, wd):
    h = modulate(rmsnorm(x, g), shift, scale)
    return x + gate * moe(h, w_router, b_router, wg, wu, wd)


def setup_inputs(seed: int = 0) -> dict:
    key = jax.random.key(seed)
    ks = jax.random.split(key, 32)
    f32 = jnp.float32
    n_pages = PAST_LEN // PAGE_SIZE
    n_used = DEC_BATCH * n_pages
    n_pool = n_used + n_used // 4
    nrm = lambda k, shape, s: jax.random.normal(k, shape, f32) * s
    page_table = jax.random.permutation(ks[5], n_pool)[:n_used].reshape(DEC_BATCH, n_pages).astype(jnp.int32)
    return {
        'x_prompt': nrm(ks[0], (BATCH, SEQ, D_MODEL), 1.0),
        'x_sample': nrm(ks[1], (DEC_BATCH, DEC_SEQ, D_MODEL), 1.0),
        'cache_k': nrm(ks[2], (DEPTH, n_pool, PAGE_SIZE, SB_HEADS, HEAD_DIM), 1.0),
        'cache_v': nrm(ks[3], (DEPTH, n_pool, PAGE_SIZE, SB_HEADS, HEAD_DIM), 1.0),
        'state_conv': nrm(ks[4], (DEPTH, DEC_BATCH, CONV_WIDTH - 1, CONV_CH), 0.5),
        'page_table': page_table,
        'c_prompt': nrm(ks[6], (BATCH, D_MODEL), 1.0),
        'c_sample': nrm(ks[7], (DEC_BATCH, D_MODEL), 1.0),
        'w_ada': nrm(ks[8], (DEPTH, D_MODEL, 6 * D_MODEL), 0.5 * D_MODEL ** -0.5),
        'b_ada': nrm(ks[9], (DEPTH, 6 * D_MODEL), 0.02),
        'norm_mix': 1.0 + nrm(ks[10], (DEPTH, D_MODEL), 0.05),
        'norm_ffn': 1.0 + nrm(ks[11], (DEPTH, D_MODEL), 0.05),
        'w_in': nrm(ks[12], (DEPTH, D_MODEL, IN_W), D_MODEL ** -0.5),
        'q_norm': 1.0 + nrm(ks[13], (DEPTH, HEAD_DIM), 0.05),
        'k_norm': 1.0 + nrm(ks[14], (DEPTH, HEAD_DIM), 0.05),
        'sb_bias': SB_BIAS_INIT + nrm(ks[25], (DEPTH, SB_HEADS), 0.1),
        'conv_w': nrm(ks[15], (DEPTH, CONV_WIDTH, CONV_CH), CONV_WIDTH ** -0.5),
        'conv_b': nrm(ks[16], (DEPTH, CONV_CH), 0.02),
        'conv_ln_g': 1.0 + nrm(ks[17], (DEPTH, CONV_CH), 0.05),
        'conv_ln_b': nrm(ks[18], (DEPTH, CONV_CH), 0.02),
        'w_out': nrm(ks[19], (DEPTH, MIX_W, D_MODEL), MIX_W ** -0.5),
        'w_router': nrm(ks[20], (D_MODEL, N_EXPERTS), D_MODEL ** -0.5),
        'b_router': nrm(ks[21], (N_EXPERTS,), 0.01),
        'w_gate': nrm(ks[22], (DEPTH, N_EXPERTS, D_MODEL, D_EXPERT), D_MODEL ** -0.5),
        'w_up': nrm(ks[23], (DEPTH, N_EXPERTS, D_MODEL, D_EXPERT), D_MODEL ** -0.5),
        'w_down': nrm(ks[24], (DEPTH, N_EXPERTS, D_EXPERT, D_MODEL), D_EXPERT ** -0.5),
    }


def reference(x_prompt, x_sample, cache_k, cache_v, state_conv, page_table, c_prompt, c_sample,
              w_ada, b_ada, norm_mix, norm_ffn, w_in, q_norm, k_norm, sb_bias, conv_w, conv_b, conv_ln_g,
              conv_ln_b, w_out, w_router, b_router, w_gate, w_up, w_down):
    dec_b, n_dec = x_sample.shape[:2]
    past_len = page_table.shape[1] * cache_k.shape[2]
    q_pos_s = past_len + jnp.arange(n_dec)
    k_pos_s = jnp.arange(past_len + n_dec)
    xp, xs = x_prompt, x_sample
    kp_l, vp_l, cp_l, ks_l, vs_l, cs_l = [], [], [], [], [], []
    for l in range(DEPTH):
        sh1p, sc1p, g1p, sh2p, sc2p, g2p = adaln(c_prompt, w_ada[l], b_ada[l])
        sh1s, sc1s, g1s, sh2s, sc2s, g2s = adaln(c_sample, w_ada[l], b_ada[l])

        hp = modulate(rmsnorm(xp, norm_mix[l]), sh1p, sc1p)
        q, k, v, glu = mix_project(hp, w_in[l], q_norm[l], k_norm[l])
        att = sb_prompt(q, k, v, sb_bias[l])
        glu_ext = jnp.pad(glu, ((0, 0), (CONV_WIDTH - 1, 0), (0, 0)))
        cv = conv_module(glu_ext, conv_w[l], conv_b[l], conv_ln_g[l], conv_ln_b[l])
        xp = xp + g1p * mix_merge(att, cv, w_out[l])
        kp_l.append(k)
        vp_l.append(v)
        cp_l.append(glu_ext[:, -(CONV_WIDTH - 1):])

        hs = modulate(rmsnorm(xs, norm_mix[l]), sh1s, sc1s)
        q, k, v, glu = mix_project(hs, w_in[l], q_norm[l], k_norm[l])
        k_past = cache_k[l][page_table].reshape(dec_b, past_len, SB_HEADS, HEAD_DIM)
        v_past = cache_v[l][page_table].reshape(dec_b, past_len, SB_HEADS, HEAD_DIM)
        att = stick_breaking(q, jnp.concatenate([k_past, k], axis=1),
                             jnp.concatenate([v_past, v], axis=1), q_pos_s, k_pos_s, sb_bias[l])
        glu_ext = jnp.concatenate([state_conv[l], glu], axis=1)
        cv = conv_module(glu_ext, conv_w[l], conv_b[l], conv_ln_g[l], conv_ln_b[l])
        xs = xs + g1s * mix_merge(att, cv, w_out[l])
        ks_l.append(k)
        vs_l.append(v)
        cs_l.append(glu_ext[:, -(CONV_WIDTH - 1):])

        xp = ffn_sublayer(xp, sh2p, sc2p, g2p, norm_ffn[l], w_router, b_router, w_gate[l], w_up[l], w_down[l])
        xs = ffn_sublayer(xs, sh2s, sc2s, g2s, norm_ffn[l], w_router, b_router, w_gate[l], w_up[l], w_down[l])

    return (xp, xs, jnp.stack(kp_l), jnp.stack(vp_l), jnp.stack(cp_l),
            jnp.stack(ks_l), jnp.stack(vs_l), jnp.stack(cs_l))
```

```python
import functools

import jax
import jax.numpy as jnp
from jax import lax
from jax.experimental import pallas as pl
from jax.experimental.pallas import tpu as pltpu

EPS = 1e-6
CONV_HALO = 32
LANES = 128
VMEM_LIMIT = 48 * 1024 * 1024

f32 = jnp.float32
bf16 = jnp.bfloat16


def _cparams(*sem):
    return pltpu.CompilerParams(dimension_semantics=sem, vmem_limit_bytes=VMEM_LIMIT)


def _dot(a, b):
    return jnp.dot(a, b, preferred_element_type=f32)


def _dot_nt(a, b):
    return lax.dot_general(a, b, (((1,), (1,)), ((), ())), preferred_element_type=f32)


def _silu(x):
    return x * jax.nn.sigmoid(x)


def _hi_lo(x):
    hi = x.astype(bf16)
    lo = (x - hi.astype(f32)).astype(bf16)
    return jnp.concatenate([hi, lo], axis=1)


def _softplus(z):
    return jnp.maximum(z, 0.0) + jnp.log1p(jnp.exp(-jnp.abs(z)))


def _adaln_kernel(c_ref, w_ref, b_ref, o_ref):
    s = _silu(c_ref[...])
    o_ref[...] = _dot(s.astype(bf16), w_ref[...].astype(bf16)) + b_ref[...]


def _adaln(c_all, w_ada, b_ada, tn=1024):
    depth, d, n6 = w_ada.shape
    r = c_all.shape[0]
    return pl.pallas_call(
        _adaln_kernel,
        out_shape=jax.ShapeDtypeStruct((depth, r, n6), f32),
        grid=(depth, n6 // tn),
        in_specs=[
            pl.BlockSpec((r, d), lambda l, j: (0, 0)),
            pl.BlockSpec((None, d, tn), lambda l, j: (l, 0, j)),
            pl.BlockSpec((None, 1, tn), lambda l, j: (l, 0, j)),
        ],
        out_specs=pl.BlockSpec((None, r, tn), lambda l, j: (l, 0, j)),
        compiler_params=_cparams("parallel", "parallel"),
        name="adaln",
    )(c_all, w_ada, b_ada.reshape(depth, 1, n6))


def _mix_in_kernel(x_ref, g_ref, sh_ref, sc_ref, w_ref, qg_ref, kg_ref, seg_ref,
                   q_ref, k_ref, v_ref, kb_ref, vb_ref, glu_ref, *, att_w, conv_ch, head_dim):
    x = x_ref[...]
    xn = x * lax.rsqrt(jnp.mean(x * x, axis=-1, keepdims=True) + EPS) * g_ref[...]
    h = xn * (1.0 + sc_ref[...]) + sh_ref[...]
    u = _dot(h.astype(bf16), w_ref[...])
    q = u[:, :att_w]
    k = u[:, att_w:2 * att_w]
    v = u[:, 2 * att_w:3 * att_w]
    ca = u[:, 3 * att_w:3 * att_w + conv_ch]
    cg = u[:, 3 * att_w + conv_ch:]

    def head_norm(t, gain):
        ss = _dot(_hi_lo(t * t), seg_ref[...])
        return t * lax.rsqrt(ss * (1.0 / head_dim) + EPS) * gain

    qn = head_norm(q, qg_ref[...])
    kn = head_norm(k, kg_ref[...])
    q_ref[...] = (qn * (head_dim ** -0.5)).astype(bf16)
    k_ref[...] = kn
    v_ref[...] = v
    kb_ref[...] = kn.astype(bf16)
    vb_ref[...] = v.astype(bf16)
    glu_ref[...] = ca * jax.nn.sigmoid(cg)


def _mix_in(x, g, shift, scale, w_in_b, qg, kg, seg, *, tile, head_dim):
    n, d = x.shape
    in_w = w_in_b.shape[1]
    att_w = qg.shape[1]
    conv_ch = (in_w - 3 * att_w) // 2
    nb, r, _ = shift.shape
    tiles_per_mod = n // (nb * tile)
    mod_spec = pl.BlockSpec((None, r, d), lambda i: (i // tiles_per_mod, 0, 0))
    row = lambda w: pl.BlockSpec((tile, w), lambda i: (i, 0))
    full = lambda a: pl.BlockSpec(a.shape, lambda i: (0,) * a.ndim)
    kern = functools.partial(_mix_in_kernel, att_w=att_w, conv_ch=conv_ch, head_dim=head_dim)
    return pl.pallas_call(
        kern,
        out_shape=(
            jax.ShapeDtypeStruct((n, att_w), bf16),
            jax.ShapeDtypeStruct((n, att_w), f32),
            jax.ShapeDtypeStruct((n, att_w), f32),
            jax.ShapeDtypeStruct((n, att_w), bf16),
            jax.ShapeDtypeStruct((n, att_w), bf16),
            jax.ShapeDtypeStruct((n, conv_ch), f32),
        ),
        grid=(n // tile,),
        in_specs=[row(d), full(g), mod_spec, mod_spec, full(w_in_b), full(qg), full(kg), full(seg)],
        out_specs=(row(att_w), row(att_w), row(att_w), row(att_w), row(att_w), row(conv_ch)),
        compiler_params=_cparams("parallel"),
        name="mix_in",
    )(x, g, shift, scale, w_in_b, qg, kg, seg)


def _sb_block(qs, kb, vb, bias, cm, acc, r, mask):
    tk = kb.shape[0]
    z = _dot_nt(qs, kb) + bias
    sp = _softplus(z)
    spm = sp if mask is None else jnp.where(mask, sp, 0.0)
    c = _dot(_hi_lo(spm), cm)
    a = jnp.exp(z - sp - c[:, :tk] + r)
    if mask is not None:
        a = jnp.where(mask, a, 0.0)
    acc = acc + _dot(a.astype(bf16), vb)
    return acc, r - c[:, tk:]


def _pair_rows(x2, hd):
    first = lax.broadcasted_iota(jnp.int32, x2.shape, 1) < hd
    zero = jnp.zeros_like(x2)
    return jnp.concatenate([jnp.where(first, x2, zero), jnp.where(first, zero, x2)], axis=0)


def _sb_prompt_kernel(bias_ref, q_ref, k_ref, v_ref, cm_ref, o_ref, *, tq, hd):
    hp = pl.program_id(1)
    qi = pl.program_id(2)
    tk = tq
    qs = _pair_rows(q_ref[...], hd)
    rows = lax.broadcasted_iota(jnp.int32, (2 * tq, tk), 0)
    cols = lax.broadcasted_iota(jnp.int32, (2 * tq, tk), 1)
    bias = jnp.where(rows < tq, bias_ref[2 * hp], bias_ref[2 * hp + 1])
    cm = cm_ref[...]
    mask = cols < jnp.where(rows < tq, rows, rows - tq)
    d0 = pl.multiple_of(qi * tq, tq)
    acc = jnp.zeros((2 * tq, 2 * hd), f32)
    r = jnp.zeros((2 * tq, tk), f32)
    acc, r = _sb_block(qs, k_ref[pl.ds(d0, tk), :], v_ref[pl.ds(d0, tk), :], bias, cm, acc, r, mask)

    def body(i, carry):
        s0 = pl.multiple_of((qi - 1 - i) * tk, tk)
        return _sb_block(qs, k_ref[pl.ds(s0, tk), :], v_ref[pl.ds(s0, tk), :], bias, cm, *carry, None)

    acc, r = lax.fori_loop(0, qi, body, (acc, r))
    first = lax.broadcasted_iota(jnp.int32, (tq, 2 * hd), 1) < hd
    o_ref[...] = jnp.where(first, acc[:tq], acc[tq:]).astype(o_ref.dtype)


def _suffix_matrix(tk):
    j = lax.broadcasted_iota(jnp.int32, (2 * tk, 2 * tk), 0) % tk
    s = lax.broadcasted_iota(jnp.int32, (2 * tk, 2 * tk), 1)
    return jnp.where((s >= tk) | (j > s), 1.0, 0.0).astype(bf16)


def _sb_prompt(q, kb, vb, bias, *, batch, hd, tq=128):
    n_tot, att_w = q.shape
    n = n_tot // batch
    nq = n // tq
    pairs = att_w // (2 * hd)
    cm = _suffix_matrix(tq)
    kern = functools.partial(_sb_prompt_kernel, tq=tq, hd=hd)
    return pl.pallas_call(
        kern,
        out_shape=jax.ShapeDtypeStruct((n_tot, att_w), bf16),
        grid_spec=pltpu.PrefetchScalarGridSpec(
            num_scalar_prefetch=0,
            grid=(batch, pairs, nq),
            in_specs=[
                pl.BlockSpec(memory_space=pltpu.SMEM),
                pl.BlockSpec((tq, 2 * hd), lambda b, p, i: (b * nq + i, p)),
                pl.BlockSpec((n, 2 * hd), lambda b, p, i: (b, p)),
                pl.BlockSpec((n, 2 * hd), lambda b, p, i: (b, p)),
                pl.BlockSpec(cm.shape, lambda b, p, i: (0, 0)),
            ],
            out_specs=pl.BlockSpec((tq, 2 * hd), lambda b, p, i: (b * nq + i, p)),
        ),
        compiler_params=_cparams("parallel", "parallel", "arbitrary"),
        name="sb_prompt",
    )(bias, q, kb, vb, cm)


def _sb_sample_kernel(pt_ref, bias_ref, q_ref, kn_ref, vn_ref, cm_ref, *rest, n_pg, n_dec, n_heads, hd):
    k_refs = rest[:n_pg]
    v_refs = rest[n_pg:2 * n_pg]
    o_ref = rest[2 * n_pg]
    kbuf, vbuf, acc_ref, r_ref = rest[2 * n_pg + 1:]
    c = pl.program_id(1)
    nrow = n_dec * n_heads
    tk = kbuf.shape[0]
    att_w = n_heads * hd
    q = q_ref[...].astype(bf16)
    rows = lax.broadcasted_iota(jnp.int32, (nrow, att_w), 0)
    lanes = lax.broadcasted_iota(jnp.int32, (nrow, att_w), 1)
    head_of_row = rows % n_heads
    own = (lanes // hd) == head_of_row
    qrep = jnp.concatenate([jnp.broadcast_to(q[t:t + 1, :], (n_heads, att_w)) for t in range(n_dec)], axis=0)
    qs = jnp.where(own, qrep, jnp.zeros_like(qrep))
    rows_k = lax.broadcasted_iota(jnp.int32, (nrow, tk), 0)
    cols_k = lax.broadcasted_iota(jnp.int32, (nrow, tk), 1)
    bias = jnp.zeros((nrow, tk), f32)
    for h in range(n_heads):
        bias = jnp.where((rows_k % n_heads) == h, bias_ref[h], bias)
    cm = cm_ref[...]

    @pl.when(c == 0)
    def _():
        kbuf[...] = jnp.zeros_like(kbuf)
        vbuf[...] = jnp.zeros_like(vbuf)
        kbuf[0:n_dec, :] = kn_ref[...]
        vbuf[0:n_dec, :] = vn_ref[...]
        mask = cols_k < (rows_k // n_heads)
        acc, r = _sb_block(qs, kbuf[...].astype(bf16), vbuf[...].astype(bf16), bias, cm,
                           jnp.zeros((nrow, att_w), f32), jnp.zeros((nrow, tk), f32), mask)
        acc_ref[...] = acc
        r_ref[...] = r

    acc = acc_ref[...]
    r = r_ref[...]
    for p in reversed(range(n_pg)):
        acc, r = _sb_block(qs, k_refs[p][...].astype(bf16), v_refs[p][...].astype(bf16), bias, cm, acc, r, None)
    acc_ref[...] = acc
    r_ref[...] = r

    @pl.when(c == pl.num_programs(1) - 1)
    def _():
        picked = jnp.where(own, acc, 0.0)
        o_ref[...] = jnp.concatenate(
            [jnp.sum(picked[t * n_heads:(t + 1) * n_heads, :], axis=0, keepdims=True) for t in range(n_dec)], axis=0)


def _sb_sample(q, k_new, v_new, cache_k, cache_v, layer, page_table, bias, *, hd, n_pg=8):
    b, n_dec, att_w = q.shape
    page = cache_k.shape[2]
    n_pages = page_table.shape[1]
    n_chunks = n_pages // n_pg
    n_heads = att_w // hd
    cm = _suffix_matrix(page)
    small = pl.BlockSpec((None, n_dec, att_w), lambda i, c, pt: (i, 0, 0))

    def page_spec(p):
        return pl.BlockSpec((None, None, page, att_w),
                            lambda i, c, pt: (layer, pt[i, (n_chunks - 1 - c) * n_pg + p], 0, 0))

    kern = functools.partial(_sb_sample_kernel, n_pg=n_pg, n_dec=n_dec, n_heads=n_heads, hd=hd)
    return pl.pallas_call(
        kern,
        out_shape=jax.ShapeDtypeStruct((b, n_dec, att_w), f32),
        grid_spec=pltpu.PrefetchScalarGridSpec(
            num_scalar_prefetch=1,
            grid=(b, n_chunks),
            in_specs=[pl.BlockSpec(memory_space=pltpu.SMEM), small, small, small,
                      pl.BlockSpec(cm.shape, lambda i, c, pt: (0, 0))]
                     + [page_spec(p) for p in range(n_pg)] + [page_spec(p) for p in range(n_pg)],
            out_specs=small,
            scratch_shapes=[pltpu.VMEM((page, att_w), f32), pltpu.VMEM((page, att_w), f32),
                            pltpu.VMEM((n_dec * n_heads, att_w), f32), pltpu.VMEM((n_dec * n_heads, page), f32)],
        ),
        compiler_params=_cparams("parallel", "arbitrary"),
        name="sb_sample",
    )(page_table, bias, q, k_new, v_new, cm, *([cache_k] * n_pg), *([cache_v] * n_pg))


def _round_bf16(x):
    return x.astype(bf16).astype(f32)


def _ln_silu(y, g, b):
    yc = y - jnp.mean(y, axis=-1, keepdims=True)
    yn = yc * lax.rsqrt(jnp.mean(yc * yc, axis=-1, keepdims=True) + EPS)
    return _silu(yn * g + b)


def _conv_prompt_kernel(cur_ref, halo_ref, w_ref, b_ref, lg_ref, lb_ref, o_ref, ext_ref, *, width, chunk):
    i = pl.program_id(1)
    t = cur_ref.shape[0]
    halo = _round_bf16(halo_ref[...])
    ext_ref[0:CONV_HALO, :] = jnp.where(i == 0, jnp.zeros_like(halo), halo)
    ext_ref[CONV_HALO:, :] = _round_bf16(cur_ref[...])
    off = CONV_HALO - (width - 1)
    for c in range(t // chunk):
        r0 = c * chunk
        y = jnp.zeros((chunk, cur_ref.shape[1]), f32) + b_ref[...]
        for w in range(width):
            y = y + ext_ref[r0 + off + w:r0 + off + w + chunk, :] * w_ref[w:w + 1, :]
        o_ref[r0:r0 + chunk, :] = _ln_silu(y, lg_ref[...], lb_ref[...]).astype(o_ref.dtype)


def _conv_prompt(glu, w, b, lg, lb, *, batch, tile=256, chunk=32):
    n_tot, ch = glu.shape
    n = n_tot // batch
    nt = n // tile
    width = w.shape[0]
    hb = tile // CONV_HALO
    vec = pl.BlockSpec((1, ch), lambda bi, i: (0, 0))
    kern = functools.partial(_conv_prompt_kernel, width=width, chunk=chunk)
    return pl.pallas_call(
        kern,
        out_shape=jax.ShapeDtypeStruct((n_tot, ch), bf16),
        grid=(batch, nt),
        in_specs=[
            pl.BlockSpec((tile, ch), lambda bi, i: (bi * nt + i, 0)),
            pl.BlockSpec((CONV_HALO, ch), lambda bi, i: (jnp.maximum((bi * nt + i) * hb - 1, 0), 0)),
            pl.BlockSpec((width, ch), lambda bi, i: (0, 0)),
            vec, vec, vec,
        ],
        out_specs=pl.BlockSpec((tile, ch), lambda bi, i: (bi * nt + i, 0)),
        scratch_shapes=[pltpu.VMEM((tile + CONV_HALO, ch), f32)],
        compiler_params=_cparams("parallel", "arbitrary"),
        name="conv_prompt",
    )(glu, glu, w, b.reshape(1, ch), lg.reshape(1, ch), lb.reshape(1, ch))


def _conv_sample_kernel(ext_ref, w_ref, b_ref, lg_ref, lb_ref, o_ref, *, width):
    n_dec = o_ref.shape[0]
    for n in range(n_dec):
        y = jnp.zeros(o_ref.shape[1:], f32) + b_ref[...]
        for w in range(width):
            y = y + _round_bf16(ext_ref[n + w]) * w_ref[w:w + 1, :]
        o_ref[n] = _ln_silu(y, lg_ref[...], lb_ref[...]).astype(o_ref.dtype)


def _conv_sample(ext_t, w, b, lg, lb, *, n_dec):
    _, bsz, ch = ext_t.shape
    width = w.shape[0]
    kern = functools.partial(_conv_sample_kernel, width=width)
    return pl.pallas_call(
        kern,
        out_shape=jax.ShapeDtypeStruct((n_dec, bsz, ch), f32),
        name="conv_sample",
    )(ext_t, w, b.reshape(1, ch), lg.reshape(1, ch), lb.reshape(1, ch))


def _route(logit_rows, b_rows, n_groups):
    n_e = len(logit_rows)
    per = n_e // n_groups
    m = functools.reduce(jnp.maximum, logit_rows)
    p = [jnp.exp(x - m) for x in logit_rows]
    inv = 1.0 / functools.reduce(jnp.add, p)
    probs = [x * inv for x in p]
    sel = [x + b for x, b in zip(probs, b_rows)]
    assert per == 4
    score = []
    for g in range(n_groups):
        a, b, c, d = sel[per * g:per * g + per]
        hi1, lo1, hi2, lo2 = jnp.maximum(a, b), jnp.minimum(a, b), jnp.maximum(c, d), jnp.minimum(c, d)
        score.append(jnp.maximum(hi1, hi2) + jnp.maximum(jnp.minimum(hi1, hi2), jnp.maximum(lo1, lo2)))
    best = jnp.zeros_like(m, dtype=jnp.int32)
    best_s = score[0]
    for g in range(1, n_groups):
        gt = score[g] > best_s
        best_s = jnp.where(gt, score[g], best_s)
        best = jnp.where(gt, g, best)

    def pick(vals, j):
        out = vals[j]
        for g in range(1, n_groups):
            out = jnp.where(best == g, vals[per * g + j], out)
        return out

    sv = [pick(sel, j) for j in range(per)]
    pv = [pick(probs, j) for j in range(per)]

    def first_argmax(vals):
        idx = jnp.zeros_like(best)
        mv = vals[0]
        for j in range(1, per):
            gt = vals[j] > mv
            mv = jnp.where(gt, vals[j], mv)
            idx = jnp.where(gt, j, idx)
        return idx

    i1 = first_argmax(sv)
    i2 = first_argmax([jnp.where(i1 == j, -jnp.inf, sv[j]) for j in range(per)])

    def take(vals, idx):
        out = vals[0]
        for j in range(1, per):
            out = jnp.where(idx == j, vals[j], out)
        return out

    p1, p2 = take(pv, i1), take(pv, i2)
    inv2 = 1.0 / (p1 + p2)
    w1, w2 = p1 * inv2, p2 * inv2
    g1, g2 = best * per + i1, best * per + i2
    return [jnp.where(g1 == e, w1, 0.0) + jnp.where(g2 == e, w2, 0.0) for e in range(n_e)]


def _mix_out_kernel(br_ref, x_ref, att_ref, cv_ref, w_ref, g1_ref, ng_ref, sh_ref, sc_ref, wr_ref,
                    x1_ref, h_ref, gates_ref, *, att_w, n_experts, n_groups):
    mix = _dot(att_ref[...].astype(bf16), w_ref[0:att_w, :]) + _dot(cv_ref[...].astype(bf16), w_ref[att_w:, :])
    x1 = x_ref[...] + g1_ref[...] * mix
    x1_ref[...] = x1
    xn = x1 * lax.rsqrt(jnp.mean(x1 * x1, axis=-1, keepdims=True) + EPS) * ng_ref[...]
    h = xn * (1.0 + sc_ref[...]) + sh_ref[...]
    hb = h.astype(bf16)
    h_ref[...] = hb
    logits = _dot(hb, wr_ref[...])
    lt = logits.T
    rows = [lt[e:e + 1, :] for e in range(n_experts)]
    gate_rows = _route(rows, [br_ref[e] for e in range(n_experts)], n_groups)
    t = lt.shape[1]
    gt = jnp.concatenate(gate_rows + [jnp.zeros((LANES - n_experts, t), f32)], axis=0)
    gates_ref[...] = gt.T


def _mix_out(x, att, cv, w_out_b, gate1, ng, shift2, scale2, wr_b, b_router, *, tile, n_groups):
    n, d = x.shape
    att_w = att.shape[1]
    conv_ch = cv.shape[1]
    n_experts = b_router.shape[0]
    nb, r, _ = gate1.shape
    tiles_per_mod = n // (nb * tile)
    mod_spec = pl.BlockSpec((None, r, d), lambda i: (i // tiles_per_mod, 0, 0))
    row = lambda w: pl.BlockSpec((tile, w), lambda i: (i, 0))
    full = lambda a: pl.BlockSpec(a.shape, lambda i: (0,) * a.ndim)
    kern = functools.partial(_mix_out_kernel, att_w=att_w, n_experts=n_experts, n_groups=n_groups)
    return pl.pallas_call(
        kern,
        out_shape=(jax.ShapeDtypeStruct((n, d), f32), jax.ShapeDtypeStruct((n, d), bf16),
                   jax.ShapeDtypeStruct((n, LANES), f32)),
        grid=(n // tile,),
        in_specs=[pl.BlockSpec(memory_space=pltpu.SMEM), row(d), row(att_w), row(conv_ch), full(w_out_b),
                  mod_spec, full(ng), mod_spec, mod_spec, full(wr_b)],
        out_specs=(row(d), row(d), row(LANES)),
        compiler_params=_cparams("parallel"),
        name="mix_out",
    )(b_router, x, att, cv, w_out_b, gate1, ng, shift2, scale2, wr_b)


def _moe_kernel(x1_ref, h_ref, gates_ref, g2_ref, wg_ref, wu_ref, wd_ref, o_ref, acc_ref):
    e = pl.program_id(1)

    @pl.when(e == 0)
    def _():
        acc_ref[...] = jnp.zeros_like(acc_ref)

    h = h_ref[...]
    a = _dot(h, wg_ref[...])
    u = _dot(h, wu_ref[...])
    gates = gates_ref[...]
    lane = lax.broadcasted_iota(jnp.int32, gates.shape, 1)
    ge = jnp.sum(jnp.where(lane == e, gates, 0.0), axis=1, keepdims=True)
    act = _silu(a) * u * ge
    acc_ref[...] += _dot(act.astype(bf16), wd_ref[...])

    @pl.when(e == pl.num_programs(1) - 1)
    def _():
        o_ref[...] = x1_ref[...] + g2_ref[...] * acc_ref[...]


def _moe(x1, h, gates, gate2, wg_b, wu_b, wd_b, *, tile):
    n, d = x1.shape
    n_e, _, f = wg_b.shape
    nb, r, _ = gate2.shape
    tiles_per_mod = n // (nb * tile)
    row = lambda w: pl.BlockSpec((tile, w), lambda i, e: (i, 0))
    return pl.pallas_call(
        _moe_kernel,
        out_shape=jax.ShapeDtypeStruct((n, d), f32),
        grid=(n // tile, n_e),
        in_specs=[row(d), row(d), row(LANES),
                  pl.BlockSpec((None, r, d), lambda i, e: (i // tiles_per_mod, 0, 0)),
                  pl.BlockSpec((None, d, f), lambda i, e: (e, 0, 0)),
                  pl.BlockSpec((None, d, f), lambda i, e: (e, 0, 0)),
                  pl.BlockSpec((None, f, d), lambda i, e: (e, 0, 0))],
        out_specs=row(d),
        scratch_shapes=[pltpu.VMEM((tile, d), f32)],
        compiler_params=_cparams("parallel", "arbitrary"),
        name="moe",
    )(x1, h, gates, gate2, wg_b, wu_b, wd_b)


def kernel(x_prompt, x_sample, cache_k, cache_v, state_conv, page_table, c_prompt, c_sample, w_ada, b_ada, norm_mix, norm_ffn, w_in, q_norm, k_norm, sb_bias, conv_w, conv_b, conv_ln_g, conv_ln_b, w_out, w_router, b_router, w_gate, w_up, w_down):
    bsz, seq, d = x_prompt.shape
    dec_b, n_dec, _ = x_sample.shape
    depth = w_ada.shape[0]
    n_heads = sb_bias.shape[1]
    hd = q_norm.shape[1]
    att_w = n_heads * hd
    conv_ch = conv_w.shape[2]
    width = conv_w.shape[1]
    n_experts = w_router.shape[1]
    n_groups = 4
    n_pool, page = cache_k.shape[1], cache_k.shape[2]
    n_s = dec_b * n_dec

    ada = _adaln(jnp.concatenate([c_prompt, c_sample], axis=0), w_ada, b_ada)

    seg = (jnp.arange(2 * att_w)[:, None] % att_w // hd == jnp.arange(att_w)[None, :] // hd).astype(bf16)
    wr_b = jnp.pad(w_router, ((0, 0), (0, LANES - n_experts))).astype(bf16)
    ck = cache_k.reshape(depth, n_pool, page, att_w)
    cv_cache = cache_v.reshape(depth, n_pool, page, att_w)

    xp = x_prompt.reshape(bsz * seq, d)
    xs = x_sample.reshape(n_s, d)
    tile_p = 256
    outs = {k: [] for k in ("kp", "vp", "cp", "ks", "vs", "cs")}
    for l in range(depth):
        mod_p = [m.reshape(bsz, 1, d) for m in jnp.split(ada[l, :bsz], 6, axis=-1)]
        mod_s = [jnp.repeat(m, n_dec, axis=0).reshape(1, n_s, d) for m in jnp.split(ada[l, bsz:], 6, axis=-1)]
        w_in_b = w_in[l].astype(bf16)
        w_out_b = w_out[l].astype(bf16)
        wg_b, wu_b, wd_b = w_gate[l].astype(bf16), w_up[l].astype(bf16), w_down[l].astype(bf16)
        nm = norm_mix[l].reshape(1, d)
        nf = norm_ffn[l].reshape(1, d)
        qg = jnp.tile(q_norm[l], n_heads).reshape(1, att_w)
        kg = jnp.tile(k_norm[l], n_heads).reshape(1, att_w)

        q, k, v, kb, vb, glu = _mix_in(xp, nm, mod_p[0], mod_p[1], w_in_b, qg, kg, seg, tile=tile_p, head_dim=hd)
        att = _sb_prompt(q, kb, vb, sb_bias[l], batch=bsz, hd=hd)
        cvp = _conv_prompt(glu, conv_w[l], conv_b[l], conv_ln_g[l], conv_ln_b[l], batch=bsz)
        x1, h2, gates = _mix_out(xp, att, cvp, w_out_b, mod_p[2], nf, mod_p[3], mod_p[4], wr_b, b_router,
                                 tile=tile_p, n_groups=n_groups)
        xp = _moe(x1, h2, gates, mod_p[5], wg_b, wu_b, wd_b, tile=512)
        outs["kp"].append(k.reshape(bsz, seq, n_heads, hd))
        outs["vp"].append(v.reshape(bsz, seq, n_heads, hd))
        outs["cp"].append(glu.reshape(bsz, seq, conv_ch)[:, seq - (width - 1):])

        q, k, v, _, _, glu = _mix_in(xs, nm, mod_s[0], mod_s[1], w_in_b, qg, kg, seg, tile=n_s, head_dim=hd)
        r3 = lambda t: t.reshape(dec_b, n_dec, -1)
        att = _sb_sample(r3(q.astype(f32)), r3(k), r3(v), ck, cv_cache, l, page_table, sb_bias[l], hd=hd)
        ext = jnp.concatenate([state_conv[l], r3(glu)], axis=1)
        cvs = _conv_sample(jnp.swapaxes(ext, 0, 1), conv_w[l], conv_b[l], conv_ln_g[l], conv_ln_b[l], n_dec=n_dec)
        cvs = jnp.swapaxes(cvs, 0, 1).reshape(n_s, conv_ch)
        x1, h2, gates = _mix_out(xs, att.reshape(n_s, att_w), cvs, w_out_b, mod_s[2], nf, mod_s[3], mod_s[4],
                                 wr_b, b_router, tile=n_s, n_groups=n_groups)
        xs = _moe(x1, h2, gates, mod_s[5], wg_b, wu_b, wd_b, tile=n_s)
        outs["ks"].append(k.reshape(dec_b, n_dec, n_heads, hd))
        outs["vs"].append(v.reshape(dec_b, n_dec, n_heads, hd))
        outs["cs"].append(ext[:, n_dec:])

    return (xp.reshape(bsz, seq, d), xs.reshape(dec_b, n_dec, d),
            jnp.stack(outs["kp"]), jnp.stack(outs["vp"]), jnp.stack(outs["cp"]),
            jnp.stack(outs["ks"]), jnp.stack(outs["vs"]), jnp.stack(outs["cs"]))
```

```python
import functools

import jax
import jax.numpy as jnp
from jax import lax
from jax.experimental import pallas as pl
from jax.experimental.pallas import tpu as pltpu

EPS = 1e-6
CONV_HALO = 32
LANES = 128
VMEM_LIMIT = 48 * 1024 * 1024

f32 = jnp.float32
bf16 = jnp.bfloat16


def _cparams(*sem):
    return pltpu.CompilerParams(dimension_semantics=sem, vmem_limit_bytes=VMEM_LIMIT)


def _dot(a, b):
    return jnp.dot(a, b, preferred_element_type=f32)


def _dot_nt(a, b):
    return lax.dot_general(a, b, (((1,), (1,)), ((), ())), preferred_element_type=f32)


def _silu(x):
    return x * jax.nn.sigmoid(x)


def _hi_lo(x):
    hi = x.astype(bf16)
    lo = (x - hi.astype(f32)).astype(bf16)
    return jnp.concatenate([hi, lo], axis=1)


def _softplus(z):
    return jnp.maximum(z, 0.0) + jnp.log(1.0 + jnp.exp(-jnp.abs(z)))


def _adaln_kernel(c_ref, w_ref, b_ref, o_ref):
    s = _silu(c_ref[...])
    o_ref[...] = _dot(s.astype(bf16), w_ref[...].astype(bf16)) + b_ref[...]


def _adaln(c_all, w_ada, b_ada, tn=1024):
    depth, d, n6 = w_ada.shape
    r = c_all.shape[0]
    return pl.pallas_call(
        _adaln_kernel,
        out_shape=jax.ShapeDtypeStruct((depth, r, n6), f32),
        grid=(depth, n6 // tn),
        in_specs=[
            pl.BlockSpec((r, d), lambda l, j: (0, 0)),
            pl.BlockSpec((None, d, tn), lambda l, j: (l, 0, j)),
            pl.BlockSpec((None, 1, tn), lambda l, j: (l, 0, j)),
        ],
        out_specs=pl.BlockSpec((None, r, tn), lambda l, j: (l, 0, j)),
        compiler_params=_cparams("parallel", "parallel"),
        name="adaln",
    )(c_all, w_ada, b_ada.reshape(depth, 1, n6))


def _mix_in_kernel(x_ref, g_ref, sh_ref, sc_ref, w_ref, qg_ref, kg_ref, seg_ref,
                   q_ref, k_ref, v_ref, kb_ref, vb_ref, glu_ref, *, att_w, conv_ch, head_dim):
    x = x_ref[...]
    xn = x * lax.rsqrt(jnp.mean(x * x, axis=-1, keepdims=True) + EPS) * g_ref[...]
    h = xn * (1.0 + sc_ref[...]) + sh_ref[...]
    u = _dot(h.astype(bf16), w_ref[...])
    q = u[:, :att_w]
    k = u[:, att_w:2 * att_w]
    v = u[:, 2 * att_w:3 * att_w]
    ca = u[:, 3 * att_w:3 * att_w + conv_ch]
    cg = u[:, 3 * att_w + conv_ch:]

    def head_norm(t, gain):
        ss = _dot(_hi_lo(t * t), seg_ref[...])
        return t * lax.rsqrt(ss * (1.0 / head_dim) + EPS) * gain

    qn = head_norm(q, qg_ref[...])
    kn = head_norm(k, kg_ref[...])
    q_ref[...] = (qn * (head_dim ** -0.5)).astype(bf16)
    k_ref[...] = kn
    v_ref[...] = v
    kb_ref[...] = kn.astype(bf16)
    vb_ref[...] = v.astype(bf16)
    glu_ref[...] = ca * jax.nn.sigmoid(cg)


def _mix_in(x, g, shift, scale, w_in_b, qg, kg, seg, *, tile, head_dim):
    n, d = x.shape
    in_w = w_in_b.shape[1]
    att_w = qg.shape[1]
    conv_ch = (in_w - 3 * att_w) // 2
    nb, r, _ = shift.shape
    tiles_per_mod = n // (nb * tile)
    mod_spec = pl.BlockSpec((None, r, d), lambda i: (i // tiles_per_mod, 0, 0))
    row = lambda w: pl.BlockSpec((tile, w), lambda i: (i, 0))
    full = lambda a: pl.BlockSpec(a.shape, lambda i: (0,) * a.ndim)
    kern = functools.partial(_mix_in_kernel, att_w=att_w, conv_ch=conv_ch, head_dim=head_dim)
    return pl.pallas_call(
        kern,
        out_shape=(
            jax.ShapeDtypeStruct((n, att_w), bf16),
            jax.ShapeDtypeStruct((n, att_w), f32),
            jax.ShapeDtypeStruct((n, att_w), f32),
            jax.ShapeDtypeStruct((n, att_w), bf16),
            jax.ShapeDtypeStruct((n, att_w), bf16),
            jax.ShapeDtypeStruct((n, conv_ch), f32),
        ),
        grid=(n // tile,),
        in_specs=[row(d), full(g), mod_spec, mod_spec, full(w_in_b), full(qg), full(kg), full(seg)],
        out_specs=(row(att_w), row(att_w), row(att_w), row(att_w), row(att_w), row(conv_ch)),
        compiler_params=_cparams("parallel"),
        name="mix_in",
    )(x, g, shift, scale, w_in_b, qg, kg, seg)


def _sb_block(qs, kb, vb, add_bias, cm, acc, r, mask):
    tk = kb.shape[0]
    sub = cm.shape[0] // 2
    z = add_bias(_dot_nt(qs, kb))
    sp = _softplus(z)
    spm = sp if mask is None else jnp.where(mask, sp, 0.0)
    e0 = z - sp
    a_parts = []
    for u in reversed(range(tk // sub)):
        c = _dot(_hi_lo(spm[:, u * sub:(u + 1) * sub]), cm)
        a_parts.append(jnp.exp(e0[:, u * sub:(u + 1) * sub] - c[:, :sub] + r))
        r = r - c[:, sub:]
    a = a_parts[0] if len(a_parts) == 1 else jnp.concatenate(a_parts[::-1], axis=1)
    if mask is not None:
        a = jnp.where(mask, a, 0.0)
    acc = acc + _dot(a.astype(bf16), vb)
    return acc, r


def _pair_rows(x2, hd):
    first = lax.broadcasted_iota(jnp.int32, x2.shape, 1) < hd
    zero = jnp.zeros_like(x2)
    return jnp.concatenate([jnp.where(first, x2, zero), jnp.where(first, zero, x2)], axis=0)


def _sb_prompt_kernel(bias_ref, q_ref, k_ref, v_ref, cm_ref, o_ref, *, tq, tk, hd):
    hp = pl.program_id(1)
    qi = pl.program_id(2)
    sub = cm_ref.shape[0] // 2
    n_diag = tq // tk
    qs = _pair_rows(q_ref[...], hd)
    b0 = bias_ref[2 * hp]
    b1 = bias_ref[2 * hp + 1]
    add_bias = lambda s: jnp.concatenate([s[:tq] + b0, s[tq:] + b1], axis=0)
    cm = cm_ref[...]
    rows = lax.broadcasted_iota(jnp.int32, (2 * tq, tk), 0)
    cols = lax.broadcasted_iota(jnp.int32, (2 * tq, tk), 1)
    row_in_tile = jnp.where(rows < tq, rows, rows - tq)
    acc = jnp.zeros((2 * tq, 2 * hd), f32)
    r = jnp.zeros((2 * tq, sub), f32)
    for g in reversed(range(n_diag)):
        d0 = pl.multiple_of(qi * tq + g * tk, tk)
        acc, r = _sb_block(qs, k_ref[pl.ds(d0, tk), :], v_ref[pl.ds(d0, tk), :], add_bias, cm, acc, r,
                           cols + g * tk < row_in_tile)

    def body(i, carry):
        s0 = pl.multiple_of((qi * n_diag - 1 - i) * tk, tk)
        return _sb_block(qs, k_ref[pl.ds(s0, tk), :], v_ref[pl.ds(s0, tk), :], add_bias, cm, *carry, None)

    acc, r = lax.fori_loop(0, qi * n_diag, body, (acc, r))
    first = lax.broadcasted_iota(jnp.int32, (tq, 2 * hd), 1) < hd
    o_ref[...] = jnp.where(first, acc[:tq], acc[tq:]).astype(o_ref.dtype)


def _suffix_matrix(tk):
    j = lax.broadcasted_iota(jnp.int32, (2 * tk, 2 * tk), 0) % tk
    s = lax.broadcasted_iota(jnp.int32, (2 * tk, 2 * tk), 1)
    return jnp.where((s >= tk) | (j > s), 1.0, 0.0).astype(bf16)


def _sb_prompt(q, kb, vb, bias, *, batch, hd, tq=512, tk=256, sub=128):
    n_tot, att_w = q.shape
    n = n_tot // batch
    nq = n // tq
    pairs = att_w // (2 * hd)
    cm = _suffix_matrix(sub)
    kern = functools.partial(_sb_prompt_kernel, tq=tq, tk=tk, hd=hd)
    return pl.pallas_call(
        kern,
        out_shape=jax.ShapeDtypeStruct((n_tot, att_w), bf16),
        grid_spec=pltpu.PrefetchScalarGridSpec(
            num_scalar_prefetch=0,
            grid=(batch, pairs, nq),
            in_specs=[
                pl.BlockSpec(memory_space=pltpu.SMEM),
                pl.BlockSpec((tq, 2 * hd), lambda b, p, i: (b * nq + i, p)),
                pl.BlockSpec((n, 2 * hd), lambda b, p, i: (b, p)),
                pl.BlockSpec((n, 2 * hd), lambda b, p, i: (b, p)),
                pl.BlockSpec(cm.shape, lambda b, p, i: (0, 0)),
            ],
            out_specs=pl.BlockSpec((tq, 2 * hd), lambda b, p, i: (b * nq + i, p)),
        ),
        compiler_params=_cparams("parallel", "parallel", "arbitrary"),
        name="sb_prompt",
    )(bias, q, kb, vb, cm)


def _sb_sample_kernel(pt_ref, bias_ref, q_ref, kn_ref, vn_ref, cm_ref, *rest, n_pg, n_dec, n_heads, hd):
    k_refs = rest[:n_pg]
    v_refs = rest[n_pg:2 * n_pg]
    o_ref = rest[2 * n_pg]
    kbuf, vbuf, acc_ref, r_ref = rest[2 * n_pg + 1:]
    c = pl.program_id(1)
    nrow = n_dec * n_heads
    tk = kbuf.shape[0]
    att_w = n_heads * hd
    q = q_ref[...].astype(bf16)
    rows = lax.broadcasted_iota(jnp.int32, (nrow, att_w), 0)
    lanes = lax.broadcasted_iota(jnp.int32, (nrow, att_w), 1)
    head_of_row = rows % n_heads
    own = (lanes // hd) == head_of_row
    qrep = jnp.concatenate([jnp.broadcast_to(q[t:t + 1, :], (n_heads, att_w)) for t in range(n_dec)], axis=0)
    qs = jnp.where(own, qrep, jnp.zeros_like(qrep))
    rows_k = lax.broadcasted_iota(jnp.int32, (nrow, tk), 0)
    cols_k = lax.broadcasted_iota(jnp.int32, (nrow, tk), 1)
    bias = jnp.zeros((nrow, tk), f32)
    for h in range(n_heads):
        bias = jnp.where((rows_k % n_heads) == h, bias_ref[h], bias)
    add_bias = lambda s: s + bias
    cm = cm_ref[...]

    @pl.when(c == 0)
    def _():
        kbuf[...] = jnp.zeros_like(kbuf)
        vbuf[...] = jnp.zeros_like(vbuf)
        kbuf[0:n_dec, :] = kn_ref[...]
        vbuf[0:n_dec, :] = vn_ref[...]
        mask = cols_k < (rows_k // n_heads)
        acc, r = _sb_block(qs, kbuf[...].astype(bf16), vbuf[...].astype(bf16), add_bias, cm,
                           jnp.zeros((nrow, att_w), f32), jnp.zeros((nrow, tk), f32), mask)
        acc_ref[...] = acc
        r_ref[...] = r

    def heads_to_lanes(ref):
        return jnp.concatenate([ref[pl.ds(h, tk, stride=n_heads), :] for h in range(n_heads)], axis=1).astype(bf16)

    acc = acc_ref[...]
    r = r_ref[...]
    for p in reversed(range(n_pg)):
        acc, r = _sb_block(qs, heads_to_lanes(k_refs[p]), heads_to_lanes(v_refs[p]), add_bias, cm, acc, r, None)
    acc_ref[...] = acc
    r_ref[...] = r

    @pl.when(c == pl.num_programs(1) - 1)
    def _():
        picked = jnp.where(own, acc, 0.0)
        o_ref[...] = jnp.concatenate(
            [jnp.sum(picked[t * n_heads:(t + 1) * n_heads, :], axis=0, keepdims=True) for t in range(n_dec)], axis=0)


def _sb_sample(q, k_new, v_new, cache_k, cache_v, layer, page_table, bias, *, hd, n_pg=8):
    b, n_dec, att_w = q.shape
    n_heads = att_w // hd
    page = cache_k.shape[2] // n_heads
    n_pages = page_table.shape[1]
    n_chunks = n_pages // n_pg
    cm = _suffix_matrix(page)
    small = pl.BlockSpec((None, n_dec, att_w), lambda i, c, pt: (i, 0, 0))

    def page_spec(p):
        return pl.BlockSpec((None, None, page * n_heads, hd),
                            lambda i, c, pt: (layer, pt[i, (n_chunks - 1 - c) * n_pg + p], 0, 0))

    kern = functools.partial(_sb_sample_kernel, n_pg=n_pg, n_dec=n_dec, n_heads=n_heads, hd=hd)
    return pl.pallas_call(
        kern,
        out_shape=jax.ShapeDtypeStruct((b, n_dec, att_w), f32),
        grid_spec=pltpu.PrefetchScalarGridSpec(
            num_scalar_prefetch=1,
            grid=(b, n_chunks),
            in_specs=[pl.BlockSpec(memory_space=pltpu.SMEM), small, small, small,
                      pl.BlockSpec(cm.shape, lambda i, c, pt: (0, 0))]
                     + [page_spec(p) for p in range(n_pg)] + [page_spec(p) for p in range(n_pg)],
            out_specs=small,
            scratch_shapes=[pltpu.VMEM((page, att_w), f32), pltpu.VMEM((page, att_w), f32),
                            pltpu.VMEM((n_dec * n_heads, att_w), f32), pltpu.VMEM((n_dec * n_heads, page), f32)],
        ),
        compiler_params=_cparams("parallel", "arbitrary"),
        name="sb_sample",
    )(page_table, bias, q, k_new, v_new, cm, *([cache_k] * n_pg), *([cache_v] * n_pg))


def _round_bf16(x):
    return x.astype(bf16).astype(f32)


def _ln_silu(y, g, b):
    yc = y - jnp.mean(y, axis=-1, keepdims=True)
    yn = yc * lax.rsqrt(jnp.mean(yc * yc, axis=-1, keepdims=True) + EPS)
    return _silu(yn * g + b)


def _conv_prompt_kernel(cur_ref, halo_ref, w_ref, b_ref, lg_ref, lb_ref, o_ref, ext_ref, *, width, chunk):
    i = pl.program_id(1)
    t = cur_ref.shape[0]
    halo = _round_bf16(halo_ref[...])
    ext_ref[0:CONV_HALO, :] = jnp.where(i == 0, jnp.zeros_like(halo), halo)
    ext_ref[CONV_HALO:, :] = _round_bf16(cur_ref[...])
    off = CONV_HALO - (width - 1)
    for c in range(t // chunk):
        r0 = c * chunk
        y = jnp.zeros((chunk, cur_ref.shape[1]), f32) + b_ref[...]
        for w in range(width):
            y = y + ext_ref[r0 + off + w:r0 + off + w + chunk, :] * w_ref[w:w + 1, :]
        o_ref[r0:r0 + chunk, :] = _ln_silu(y, lg_ref[...], lb_ref[...]).astype(o_ref.dtype)


def _conv_prompt(glu, w, b, lg, lb, *, batch, tile=256, chunk=32):
    n_tot, ch = glu.shape
    n = n_tot // batch
    nt = n // tile
    width = w.shape[0]
    hb = tile // CONV_HALO
    vec = pl.BlockSpec((1, ch), lambda bi, i: (0, 0))
    kern = functools.partial(_conv_prompt_kernel, width=width, chunk=chunk)
    return pl.pallas_call(
        kern,
        out_shape=jax.ShapeDtypeStruct((n_tot, ch), bf16),
        grid=(batch, nt),
        in_specs=[
            pl.BlockSpec((tile, ch), lambda bi, i: (bi * nt + i, 0)),
            pl.BlockSpec((CONV_HALO, ch), lambda bi, i: (jnp.maximum((bi * nt + i) * hb - 1, 0), 0)),
            pl.BlockSpec((width, ch), lambda bi, i: (0, 0)),
            vec, vec, vec,
        ],
        out_specs=pl.BlockSpec((tile, ch), lambda bi, i: (bi * nt + i, 0)),
        scratch_shapes=[pltpu.VMEM((tile + CONV_HALO, ch), f32)],
        compiler_params=_cparams("parallel", "arbitrary"),
        name="conv_prompt",
    )(glu, glu, w, b.reshape(1, ch), lg.reshape(1, ch), lb.reshape(1, ch))


def _conv_sample_kernel(ext_ref, w_ref, b_ref, lg_ref, lb_ref, o_ref, *, width):
    n_dec = o_ref.shape[0]
    for n in range(n_dec):
        y = jnp.zeros(o_ref.shape[1:], f32) + b_ref[...]
        for w in range(width):
            y = y + _round_bf16(ext_ref[n + w]) * w_ref[w:w + 1, :]
        o_ref[n] = _ln_silu(y, lg_ref[...], lb_ref[...]).astype(o_ref.dtype)


def _conv_sample(ext_t, w, b, lg, lb, *, n_dec):
    _, bsz, ch = ext_t.shape
    width = w.shape[0]
    kern = functools.partial(_conv_sample_kernel, width=width)
    return pl.pallas_call(
        kern,
        out_shape=jax.ShapeDtypeStruct((n_dec, bsz, ch), f32),
        name="conv_sample",
    )(ext_t, w, b.reshape(1, ch), lg.reshape(1, ch), lb.reshape(1, ch))


def _route(logit_rows, b_rows, n_groups):
    n_e = len(logit_rows)
    per = n_e // n_groups
    m = functools.reduce(jnp.maximum, logit_rows)
    p = [jnp.exp(x - m) for x in logit_rows]
    inv = 1.0 / functools.reduce(jnp.add, p)
    probs = [x * inv for x in p]
    sel = [x + b for x, b in zip(probs, b_rows)]
    assert per == 4
    score = []
    for g in range(n_groups):
        a, b, c, d = sel[per * g:per * g + per]
        hi1, lo1, hi2, lo2 = jnp.maximum(a, b), jnp.minimum(a, b), jnp.maximum(c, d), jnp.minimum(c, d)
        score.append(jnp.maximum(hi1, hi2) + jnp.maximum(jnp.minimum(hi1, hi2), jnp.maximum(lo1, lo2)))
    best = jnp.zeros_like(m, dtype=jnp.int32)
    best_s = score[0]
    for g in range(1, n_groups):
        gt = score[g] > best_s
        best_s = jnp.where(gt, score[g], best_s)
        best = jnp.where(gt, g, best)

    def pick(vals, j):
        out = vals[j]
        for g in range(1, n_groups):
            out = jnp.where(best == g, vals[per * g + j], out)
        return out

    sv = [pick(sel, j) for j in range(per)]
    pv = [pick(probs, j) for j in range(per)]

    def first_argmax(vals):
        idx = jnp.zeros_like(best)
        mv = vals[0]
        for j in range(1, per):
            gt = vals[j] > mv
            mv = jnp.where(gt, vals[j], mv)
            idx = jnp.where(gt, j, idx)
        return idx

    i1 = first_argmax(sv)
    i2 = first_argmax([jnp.where(i1 == j, -jnp.inf, sv[j]) for j in range(per)])

    def take(vals, idx):
        out = vals[0]
        for j in range(1, per):
            out = jnp.where(idx == j, vals[j], out)
        return out

    p1, p2 = take(pv, i1), take(pv, i2)
    inv2 = 1.0 / (p1 + p2)
    w1, w2 = p1 * inv2, p2 * inv2
    g1, g2 = best * per + i1, best * per + i2
    return [jnp.where(g1 == e, w1, 0.0) + jnp.where(g2 == e, w2, 0.0) for e in range(n_e)]


def _mix_out_kernel(br_ref, x_ref, att_ref, cv_ref, w_ref, g1_ref, ng_ref, sh_ref, sc_ref, wr_ref,
                    x1_ref, h_ref, gates_ref, *, att_w, n_experts, n_groups):
    mix = _dot(att_ref[...].astype(bf16), w_ref[0:att_w, :]) + _dot(cv_ref[...].astype(bf16), w_ref[att_w:, :])
    x1 = x_ref[...] + g1_ref[...] * mix
    x1_ref[...] = x1
    xn = x1 * lax.rsqrt(jnp.mean(x1 * x1, axis=-1, keepdims=True) + EPS) * ng_ref[...]
    h = xn * (1.0 + sc_ref[...]) + sh_ref[...]
    hb = h.astype(bf16)
    h_ref[...] = hb
    logits = _dot(hb, wr_ref[...])
    lt = logits.T
    rows = [lt[e:e + 1, :] for e in range(n_experts)]
    gate_rows = _route(rows, [br_ref[e] for e in range(n_experts)], n_groups)
    t = lt.shape[1]
    gt = jnp.concatenate(gate_rows + [jnp.zeros((LANES - n_experts, t), f32)], axis=0)
    gates_ref[...] = gt.T


def _mix_out(x, att, cv, w_out_b, gate1, ng, shift2, scale2, wr_b, b_router, *, tile, n_groups):
    n, d = x.shape
    att_w = att.shape[1]
    conv_ch = cv.shape[1]
    n_experts = b_router.shape[0]
    nb, r, _ = gate1.shape
    tiles_per_mod = n // (nb * tile)
    mod_spec = pl.BlockSpec((None, r, d), lambda i: (i // tiles_per_mod, 0, 0))
    row = lambda w: pl.BlockSpec((tile, w), lambda i: (i, 0))
    full = lambda a: pl.BlockSpec(a.shape, lambda i: (0,) * a.ndim)
    kern = functools.partial(_mix_out_kernel, att_w=att_w, n_experts=n_experts, n_groups=n_groups)
    return pl.pallas_call(
        kern,
        out_shape=(jax.ShapeDtypeStruct((n, d), f32), jax.ShapeDtypeStruct((n, d), bf16),
                   jax.ShapeDtypeStruct((n, LANES), f32)),
        grid=(n // tile,),
        in_specs=[pl.BlockSpec(memory_space=pltpu.SMEM), row(d), row(att_w), row(conv_ch), full(w_out_b),
                  mod_spec, full(ng), mod_spec, mod_spec, full(wr_b)],
        out_specs=(row(d), row(d), row(LANES)),
        compiler_params=_cparams("parallel"),
        name="mix_out",
    )(b_router, x, att, cv, w_out_b, gate1, ng, shift2, scale2, wr_b)


def _moe_kernel(x1_ref, h_ref, gates_ref, g2_ref, wg_ref, wu_ref, wd_ref, o_ref, acc_ref):
    e = pl.program_id(1)

    @pl.when(e == 0)
    def _():
        acc_ref[...] = jnp.zeros_like(acc_ref)

    h = h_ref[...]
    a = _dot(h, wg_ref[...])
    u = _dot(h, wu_ref[...])
    gates = gates_ref[...]
    lane = lax.broadcasted_iota(jnp.int32, gates.shape, 1)
    ge = jnp.sum(jnp.where(lane == e, gates, 0.0), axis=1, keepdims=True)
    act = _silu(a) * u * ge
    acc_ref[...] += _dot(act.astype(bf16), wd_ref[...])

    @pl.when(e == pl.num_programs(1) - 1)
    def _():
        o_ref[...] = x1_ref[...] + g2_ref[...] * acc_ref[...]


def _moe(x1, h, gates, gate2, wg_b, wu_b, wd_b, *, tile):
    n, d = x1.shape
    n_e, _, f = wg_b.shape
    nb, r, _ = gate2.shape
    tiles_per_mod = n // (nb * tile)
    row = lambda w: pl.BlockSpec((tile, w), lambda i, e: (i, 0))
    return pl.pallas_call(
        _moe_kernel,
        out_shape=jax.ShapeDtypeStruct((n, d), f32),
        grid=(n // tile, n_e),
        in_specs=[row(d), row(d), row(LANES),
                  pl.BlockSpec((None, r, d), lambda i, e: (i // tiles_per_mod, 0, 0)),
                  pl.BlockSpec((None, d, f), lambda i, e: (e, 0, 0)),
                  pl.BlockSpec((None, d, f), lambda i, e: (e, 0, 0)),
                  pl.BlockSpec((None, f, d), lambda i, e: (e, 0, 0))],
        out_specs=row(d),
        scratch_shapes=[pltpu.VMEM((tile, d), f32)],
        compiler_params=_cparams("parallel", "arbitrary"),
        name="moe",
    )(x1, h, gates, gate2, wg_b, wu_b, wd_b)


def kernel(x_prompt, x_sample, cache_k, cache_v, state_conv, page_table, c_prompt, c_sample, w_ada, b_ada, norm_mix, norm_ffn, w_in, q_norm, k_norm, sb_bias, conv_w, conv_b, conv_ln_g, conv_ln_b, w_out, w_router, b_router, w_gate, w_up, w_down):
    bsz, seq, d = x_prompt.shape
    dec_b, n_dec, _ = x_sample.shape
    depth = w_ada.shape[0]
    n_heads = sb_bias.shape[1]
    hd = q_norm.shape[1]
    att_w = n_heads * hd
    conv_ch = conv_w.shape[2]
    width = conv_w.shape[1]
    n_experts = w_router.shape[1]
    n_groups = 4
    n_pool, page = cache_k.shape[1], cache_k.shape[2]
    n_s = dec_b * n_dec

    ada = _adaln(jnp.concatenate([c_prompt, c_sample], axis=0), w_ada, b_ada)

    seg = (jnp.arange(2 * att_w)[:, None] % att_w // hd == jnp.arange(att_w)[None, :] // hd).astype(bf16)
    wr_b = jnp.pad(w_router, ((0, 0), (0, LANES - n_experts))).astype(bf16)
    ck = cache_k.reshape(depth, n_pool, page * n_heads, hd)
    cv_cache = cache_v.reshape(depth, n_pool, page * n_heads, hd)

    xp = x_prompt.reshape(bsz * seq, d)
    xs = x_sample.reshape(n_s, d)
    tile_p = 256
    outs = {k: [] for k in ("kp", "vp", "cp", "ks", "vs", "cs")}
    for l in range(depth):
        mod_p = [m.reshape(bsz, 1, d) for m in jnp.split(ada[l, :bsz], 6, axis=-1)]
        mod_s = [jnp.repeat(m, n_dec, axis=0).reshape(1, n_s, d) for m in jnp.split(ada[l, bsz:], 6, axis=-1)]
        w_in_b = w_in[l].astype(bf16)
        w_out_b = w_out[l].astype(bf16)
        wg_b, wu_b, wd_b = w_gate[l].astype(bf16), w_up[l].astype(bf16), w_down[l].astype(bf16)
        nm = norm_mix[l].reshape(1, d)
        nf = norm_ffn[l].reshape(1, d)
        qg = jnp.tile(q_norm[l], n_heads).reshape(1, att_w)
        kg = jnp.tile(k_norm[l], n_heads).reshape(1, att_w)

        q, k, v, kb, vb, glu = _mix_in(xp, nm, mod_p[0], mod_p[1], w_in_b, qg, kg, seg, tile=tile_p, head_dim=hd)
        att = _sb_prompt(q, kb, vb, sb_bias[l], batch=bsz, hd=hd)
        cvp = _conv_prompt(glu, conv_w[l], conv_b[l], conv_ln_g[l], conv_ln_b[l], batch=bsz)
        x1, h2, gates = _mix_out(xp, att, cvp, w_out_b, mod_p[2], nf, mod_p[3], mod_p[4], wr_b, b_router,
                                 tile=tile_p, n_groups=n_groups)
        xp = _moe(x1, h2, gates, mod_p[5], wg_b, wu_b, wd_b, tile=512)
        outs["kp"].append(k.reshape(bsz, seq, n_heads, hd))
        outs["vp"].append(v.reshape(bsz, seq, n_heads, hd))
        outs["cp"].append(glu.reshape(bsz, seq, conv_ch)[:, seq - (width - 1):])

        q, k, v, _, _, glu = _mix_in(xs, nm, mod_s[0], mod_s[1], w_in_b, qg, kg, seg, tile=n_s, head_dim=hd)
        r3 = lambda t: t.reshape(dec_b, n_dec, -1)
        att = _sb_sample(r3(q.astype(f32)), r3(k), r3(v), ck, cv_cache, l, page_table, sb_bias[l], hd=hd)
        ext = jnp.concatenate([state_conv[l], r3(glu)], axis=1)
        cvs = _conv_sample(jnp.swapaxes(ext, 0, 1), conv_w[l], conv_b[l], conv_ln_g[l], conv_ln_b[l], n_dec=n_dec)
        cvs = jnp.swapaxes(cvs, 0, 1).reshape(n_s, conv_ch)
        x1, h2, gates = _mix_out(xs, att.reshape(n_s, att_w), cvs, w_out_b, mod_s[2], nf, mod_s[3], mod_s[4],
                                 wr_b, b_router, tile=n_s, n_groups=n_groups)
        xs = _moe(x1, h2, gates, mod_s[5], wg_b, wu_b, wd_b, tile=n_s)
        outs["ks"].append(k.reshape(dec_b, n_dec, n_heads, hd))
        outs["vs"].append(v.reshape(dec_b, n_dec, n_heads, hd))
        outs["cs"].append(ext[:, n_dec:])

    return (xp.reshape(bsz, seq, d), xs.reshape(dec_b, n_dec, d),
            jnp.stack(outs["kp"]), jnp.stack(outs["vp"]), jnp.stack(outs["cp"]),
            jnp.stack(outs["ks"]), jnp.stack(outs["vs"]), jnp.stack(outs["cs"]))
```
